```python
import math
import jax, jax.numpy as jnp
from jax import lax
import numpy as np

D_MODEL = 1024
BATCH = 16
SEQ = 4096
DEPTH = 1

PLE_DIM = 256
D_FF = 2816
GM_HEADS = 4
GM_HEAD_DIM = 128
GM_WIDTH = GM_HEADS * GM_HEAD_DIM
GM_CHUNK = 128
DN_HEADS = 4
DN_HEAD_DIM = 128
DN_WIDTH = DN_HEADS * DN_HEAD_DIM
DN_CHUNK = 64
N_DIR = 2
CONV_WIDTH = 5
MIX_WIDTH = GM_WIDTH + DN_WIDTH
EPS = 1e-6
SPLITS = (GM_WIDTH, 2 * GM_WIDTH, 2 * GM_WIDTH + 3 * DN_WIDTH, 2 * GM_WIDTH + 4 * DN_WIDTH,
          2 * GM_WIDTH + 4 * DN_WIDTH + N_DIR * DN_HEADS)
IN_COLS = 2 * GM_WIDTH + 4 * DN_WIDTH + 2 * N_DIR * DN_HEADS

kernel_name = 'hybrid_gmlp_gdn_macaron_encoder'


def rmsnorm(x, g):
    xf = x.astype(jnp.float32)
    y = xf * lax.rsqrt(jnp.mean(xf * xf, axis=-1, keepdims=True) + EPS)
    return (y * g.astype(jnp.float32)).astype(x.dtype)


def layernorm(x, g, b):
    xf = x.astype(jnp.float32)
    mu = jnp.mean(xf, axis=-1, keepdims=True)
    xc = xf - mu
    y = xc * lax.rsqrt(jnp.mean(xc * xc, axis=-1, keepdims=True) + EPS)
    return (y * g.astype(jnp.float32) + b.astype(jnp.float32)).astype(x.dtype)


def l2norm(x):
    return x * lax.rsqrt(jnp.sum(x * x, axis=-1, keepdims=True) + EPS)


def swiglu(x, w_gate, w_up, w_down):
    return (jax.nn.silu(x @ w_gate) * (x @ w_up)) @ w_down


def gmlp_mixer(zu, zv, ln_g, ln_b, ws, bs):
    bsz, s, _ = zu.shape
    u = jax.nn.gelu(zu)
    v = jax.nn.gelu(zv).reshape(bsz, s // GM_CHUNK, GM_CHUNK, GM_HEADS, GM_HEAD_DIM)
    v = layernorm(v, ln_g, ln_b)
    sv = jnp.einsum('hij,bnjhd->bnihd', ws, v) + bs.T[:, :, None]
    return u * sv.reshape(bsz, s, GM_WIDTH)


def chunk_gated_delta(q, k, v, g, beta):
    bsz, nh, s, dk = q.shape
    dv = v.shape[-1]
    nc = s // DN_CHUNK
    q, k, v, g, beta = [t.reshape(bsz, nh, nc, DN_CHUNK, *t.shape[3:]) for t in (q, k, v, g, beta)]
    g = jnp.cumsum(g, axis=-1)
    idx = jnp.arange(DN_CHUNK)
    incl = idx[:, None] >= idx[None, :]
    strict = idx[:, None] > idx[None, :]
    decay = jnp.exp(jnp.where(incl, g[..., :, None] - g[..., None, :], -jnp.inf))
    kk = jnp.einsum('bhncd,bhnmd->bhncm', k, k)
    a_mat = jnp.where(strict, beta[..., None] * kk * decay, 0.0) + jnp.eye(DN_CHUNK, dtype=jnp.float32)
    rhs = jnp.concatenate([v * beta[..., None], k * (beta * jnp.exp(g))[..., None]], axis=-1)
    sol = lax.linalg.triangular_solve(a_mat, rhs, left_side=True, lower=True, unit_diagonal=True)
    u, w = sol[..., :dv], sol[..., dv:]
    qk = jnp.where(incl, jnp.einsum('bhncd,bhnmd->bhncm', q, k) * decay, 0.0)
    q_dec = q * jnp.exp(g)[..., None]
    k_dec = k * jnp.exp(g[..., -1:] - g)[..., None]
    g_last = jnp.exp(g[..., -1])

    def step(state, inp):
        qk_c, qd_c, kd_c, u_c, w_c, gl_c = inp
        v_new = u_c - jnp.einsum('bhcd,bhde->bhce', w_c, state)
        o = jnp.einsum('bhcd,bhde->bhce', qd_c, state) + jnp.einsum('bhcm,bhme->bhce', qk_c, v_new)
        state = state * gl_c[..., None, None] + jnp.einsum('bhcd,bhce->bhde', kd_c, v_new)
        return state, o

    xs = tuple(jnp.moveaxis(t, 2, 0) for t in (qk, q_dec, k_dec, u, w, g_last))
    s0 = jnp.zeros((bsz, nh, dk, dv), jnp.float32)
    _, o = lax.scan(step, s0, xs)
    return jnp.moveaxis(o, 0, 2).reshape(bsz, nh, s, dv)


def deltanet_mixer(zqkv, zg, za, zb, conv_w, a_log, dt_bias, out_norm):
    bsz, s, _ = zqkv.shape
    f32 = jnp.float32
    qkv = lax.conv_general_dilated(zqkv, conv_w[:, None, :].astype(zqkv.dtype), window_strides=(1,),
                                   padding=[(CONV_WIDTH // 2, CONV_WIDTH // 2)],
                                   dimension_numbers=('NWC', 'WIO', 'NWC'),
                                   feature_group_count=3 * DN_WIDTH)
    qkv = jax.nn.silu(qkv).astype(f32).reshape(bsz, s, 3, DN_HEADS, DN_HEAD_DIM)
    q = l2norm(qkv[:, :, 0]) * (DN_HEAD_DIM ** -0.5)
    k = l2norm(qkv[:, :, 1])
    v = qkv[:, :, 2]
    a = za.astype(f32).reshape(bsz, s, N_DIR, DN_HEADS)
    g = -jnp.exp(a_log.astype(f32)) * jax.nn.softplus(a + dt_bias.astype(f32))
    beta = jax.nn.sigmoid(zb.astype(f32).reshape(bsz, s, N_DIR, DN_HEADS))

    def both_dirs(t):
        t2 = jnp.concatenate([t, jnp.flip(t, axis=1)], axis=2)
        return jnp.moveaxis(t2, 1, 2)

    gd = jnp.moveaxis(jnp.concatenate([g[:, :, 0], jnp.flip(g[:, :, 1], axis=1)], axis=2), 1, 2)
    bd = jnp.moveaxis(jnp.concatenate([beta[:, :, 0], jnp.flip(beta[:, :, 1], axis=1)], axis=2), 1, 2)
    o = chunk_gated_delta(both_dirs(q), both_dirs(k), both_dirs(v), gd, bd)
    o = o[:, :DN_HEADS] + jnp.flip(o[:, DN_HEADS:], axis=2)
    o = jnp.moveaxis(o, 1, 2)
    gate = zg.astype(f32).reshape(bsz, s, DN_HEADS, DN_HEAD_DIM)
    y = rmsnorm(o, out_norm) * jax.nn.silu(gate)
    return y.reshape(bsz, s, DN_WIDTH).astype(zqkv.dtype)


def setup_inputs(seed: int = 0) -> dict:
    key = jax.random.key(seed)
    ks = jax.random.split(key, 32)
    f32 = jnp.float32
    L = DEPTH

    def nrm(i, shape, scale):
        return jax.random.normal(ks[i], shape, f32) * scale

    def gain(i, shape):
        return 1.0 + nrm(i, shape, 0.05)

    x = nrm(0, (BATCH, SEQ, D_MODEL), 1.0)
    p = nrm(1, (DEPTH, BATCH, SEQ, PLE_DIM), 1.0)
    a_log = jnp.log(jax.random.uniform(ks[13], (L, N_DIR, DN_HEADS), f32, 1.0, 16.0))
    dt = jnp.exp(jax.random.uniform(ks[14], (L, N_DIR, DN_HEADS), f32, math.log(1e-3), math.log(1e-1)))
    dt_bias = dt + jnp.log(-jnp.expm1(-dt))
    return {
        'x': x,
        'p': p,
        'ffn1_norm': gain(2, (L, D_MODEL)),
        'ffn1_w_gate': nrm(3, (L, D_MODEL, D_FF), D_MODEL ** -0.5),
        'ffn1_w_up': nrm(4, (L, D_MODEL, D_FF), D_MODEL ** -0.5),
        'ffn1_w_down': nrm(5, (L, D_FF, D_MODEL), D_FF ** -0.5),
        'mix_norm': gain(6, (L, D_MODEL)),
        'w_in': nrm(7, (L, D_MODEL, IN_COLS), D_MODEL ** -0.5),
        'gm_ln_g': gain(8, (L, GM_HEADS, GM_HEAD_DIM)),
        'gm_ln_b': nrm(9, (L, GM_HEADS, GM_HEAD_DIM), 0.02),
        'gm_ws': nrm(10, (L, GM_HEADS, GM_CHUNK, GM_CHUNK), GM_CHUNK ** -0.5),
        'gm_bs': 1.0 + nrm(11, (L, GM_HEADS, GM_CHUNK), 0.1),
        'dn_conv_w': nrm(12, (L, CONV_WIDTH, 3 * DN_WIDTH), CONV_WIDTH ** -0.5),
        'dn_a_log': a_log,
        'dn_dt_bias': dt_bias,
        'dn_out_norm': gain(15, (L, DN_HEAD_DIM)),
        'w_out': nrm(16, (L, MIX_WIDTH, D_MODEL), MIX_WIDTH ** -0.5),
        'ffn2_norm': gain(17, (L, D_MODEL)),
        'ffn2_w_gate': nrm(18, (L, D_MODEL, D_FF), D_MODEL ** -0.5),
        'ffn2_w_up': nrm(19, (L, D_MODEL, D_FF), D_MODEL ** -0.5),
        'ffn2_w_down': nrm(20, (L, D_FF, D_MODEL), D_FF ** -0.5),
        'ple_proj': nrm(21, (L, PLE_DIM, D_MODEL), PLE_DIM ** -0.5),
        'ple_norm': gain(22, (L, D_MODEL)),
        'ple_gate_norm': gain(23, (L, D_MODEL)),
        'ple_gate_w': nrm(24, (L, D_MODEL, D_MODEL), D_MODEL ** -0.5),
        'ple_gate_b': nrm(25, (L, D_MODEL), 0.02),
        'final_norm': gain(26, (D_MODEL,)),
    }


def reference(x, p, ffn1_norm, ffn1_w_gate, ffn1_w_up, ffn1_w_down, mix_norm, w_in,
              gm_ln_g, gm_ln_b, gm_ws, gm_bs, dn_conv_w, dn_a_log, dn_dt_bias, dn_out_norm,
              w_out, ffn2_norm, ffn2_w_gate, ffn2_w_up, ffn2_w_down, ple_proj, ple_norm,
              ple_gate_norm, ple_gate_w, ple_gate_b, final_norm):
    h = x
    for i in range(DEPTH):
        h = h + 0.5 * swiglu(rmsnorm(h, ffn1_norm[i]), ffn1_w_gate[i], ffn1_w_up[i], ffn1_w_down[i])
        n = rmsnorm(h, mix_norm[i])
        z = n @ w_in[i]
        zu, zv, zqkv, zg, za, zb = jnp.split(z, SPLITS, axis=-1)
        y_gm = gmlp_mixer(zu, zv, gm_ln_g[i], gm_ln_b[i], gm_ws[i], gm_bs[i])
        y_dn = deltanet_mixer(zqkv, zg, za, zb, dn_conv_w[i], dn_a_log[i], dn_dt_bias[i], dn_out_norm[i])
        h = h + jnp.concatenate([y_gm, y_dn], axis=-1) @ w_out[i]
        h = h + 0.5 * swiglu(rmsnorm(h, ffn2_norm[i]), ffn2_w_gate[i], ffn2_w_up[i], ffn2_w_down[i])
        e = rmsnorm(p[i] @ ple_proj[i], ple_norm[i])
        gate = jax.nn.sigmoid(rmsnorm(h, ple_gate_norm[i]) @ ple_gate_w[i] + ple_gate_b[i])
        h = h + gate * e
    return rmsnorm(h, final_norm)
```

```python
import functools
import math

import jax
import jax.numpy as jnp
from jax import lax
from jax.experimental import pallas as pl
from jax.experimental.pallas import tpu as pltpu

F32 = jnp.float32
MXU_DTYPE = jnp.bfloat16
EPS = 1e-6

LANES = 128
HEAD_DIM = 128
GM_HEADS = 4
DN_HEADS = 4
GM_WIDTH = GM_HEADS * HEAD_DIM
DN_WIDTH = DN_HEADS * HEAD_DIM
GM_CHUNK = 128
DN_CHUNK = 128
INV_BASE = 16
CONV_WIDTH = 5
CONV_HALO = 8
VMEM_LIMIT_BYTES = 56 * 1024 * 1024

TOKEN_TILE = 256
FF_TILE = 256
GMLP_TILE = 512
PREP_TILE = 512


def _mm(a, b):
    return jnp.dot(a.astype(MXU_DTYPE), b.astype(MXU_DTYPE), preferred_element_type=F32)


def _mm_nt(a, b):
    return lax.dot_general(a.astype(MXU_DTYPE), b.astype(MXU_DTYPE), (((1,), (1,)), ((), ())),
                           preferred_element_type=F32)


def _mm_tn(a, b):
    return lax.dot_general(a.astype(MXU_DTYPE), b.astype(MXU_DTYPE), (((0,), (0,)), ((), ())),
                           preferred_element_type=F32)


def _split3(x):
    hi = x.astype(MXU_DTYPE)
    r1 = x - hi.astype(F32)
    mid = r1.astype(MXU_DTYPE)
    lo = (r1 - mid.astype(F32)).astype(MXU_DTYPE)
    return hi, mid, lo


def _rms(x, g):
    return x * lax.rsqrt(jnp.mean(x * x, axis=-1, keepdims=True) + EPS) * g


def _silu(x):
    return x * jax.nn.sigmoid(x)


def _gelu_tanh(x):
    return 0.5 * x * (1.0 + jnp.tanh(math.sqrt(2.0 / math.pi) * (x + 0.044715 * (x * x * x))))


def _softplus(x):
    return jnp.maximum(x, 0.0) + jnp.log1p(jnp.exp(-jnp.abs(x)))


def _swiglu(xn, wg_ref, wu_ref, wd_ref):
    d_ff = wg_ref.shape[1]
    acc = None
    for j in range(0, d_ff, FF_TILE):
        g = _mm(xn, wg_ref[:, j:j + FF_TILE])
        u = _mm(xn, wu_ref[:, j:j + FF_TILE])
        part = _mm(_silu(g) * u, wd_ref[j:j + FF_TILE, :])
        acc = part if acc is None else acc + part
    return acc


def _resident(shape):
    nd = len(shape)
    return pl.BlockSpec(shape, lambda *_: (0,) * nd, pipeline_mode=pl.Buffered(1))


def _params(n_axes):
    return pltpu.CompilerParams(dimension_semantics=("arbitrary",) * n_axes, vmem_limit_bytes=VMEM_LIMIT_BYTES)


def _ffn1_mix_kernel(x_ref, n1_ref, wg_ref, wu_ref, wd_ref, mn_ref, win_ref, wab_ref,
                     h_ref, zu_ref, zv_ref, zqkv_ref, zg_ref, zab_ref):
    x = x_ref[...]
    xn = _rms(x, n1_ref[...]).astype(MXU_DTYPE)
    h = x + 0.5 * _swiglu(xn, wg_ref, wu_ref, wd_ref)
    h_ref[...] = h
    n = _rms(h, mn_ref[...]).astype(MXU_DTYPE)
    c0 = 0
    for ref in (zu_ref, zv_ref, zqkv_ref, zg_ref):
        width = ref.shape[1]
        for j in range(0, width, 512):
            ref[:, j:j + 512] = _mm(n, win_ref[:, c0 + j:c0 + j + 512]).astype(ref.dtype)
        c0 += width
    zab_ref[...] = _mm(n, wab_ref[...])


def _ffn1_mix(x, n1, wg, wu, wd, mn, w_in_main, w_in_ab):
    t, d = x.shape
    tm = TOKEN_TILE
    n_ab = w_in_ab.shape[1]
    row = lambda w: pl.BlockSpec((tm, w), lambda i: (i, 0))
    outs = [(d, F32), (GM_WIDTH, F32), (GM_WIDTH, F32), (3 * DN_WIDTH, F32), (DN_WIDTH, F32), (n_ab, F32)]
    return pl.pallas_call(
        _ffn1_mix_kernel,
        grid=(t // tm,),
        in_specs=[row(d), _resident(n1.shape), _resident(wg.shape), _resident(wu.shape), _resident(wd.shape),
                  _resident(mn.shape), _resident(w_in_main.shape), _resident(w_in_ab.shape)],
        out_specs=[row(w) for w, _ in outs],
        out_shape=[jax.ShapeDtypeStruct((t, w), dt) for w, dt in outs],
        compiler_params=_params(1),
        name="ffn1_mix",
    )(x, n1, wg, wu, wd, mn, w_in_main, w_in_ab)


def _gmlp_kernel(zu_ref, zv_ref, lng_ref, lnb_ref, ws_ref, bst_ref, y_ref):
    rows = zu_ref.shape[0]
    for c in range(0, rows, GM_CHUNK):
        for h in range(GM_HEADS):
            cols = slice(h * HEAD_DIM, (h + 1) * HEAD_DIM)
            v = _gelu_tanh(zv_ref[c:c + GM_CHUNK, cols])
            xc = v - jnp.mean(v, axis=-1, keepdims=True)
            vn = xc * lax.rsqrt(jnp.mean(xc * xc, axis=-1, keepdims=True) + EPS)
            vn = vn * lng_ref[h:h + 1, :] + lnb_ref[h:h + 1, :]
            sv = _mm(ws_ref[h], vn) + bst_ref[:, h:h + 1]
            y_ref[c:c + GM_CHUNK, cols] = (_gelu_tanh(zu_ref[c:c + GM_CHUNK, cols]) * sv).astype(y_ref.dtype)


def _gmlp(zu, zv, ln_g, ln_b, ws, bs_t):
    t = zu.shape[0]
    tg = GMLP_TILE
    row = pl.BlockSpec((tg, GM_WIDTH), lambda i: (i, 0))
    return pl.pallas_call(
        _gmlp_kernel,
        grid=(t // tg,),
        in_specs=[row, row, _resident(ln_g.shape), _resident(ln_b.shape), _resident(ws.shape),
                  _resident(bs_t.shape)],
        out_specs=row,
        out_shape=jax.ShapeDtypeStruct((t, GM_WIDTH), MXU_DTYPE),
        compiler_params=_params(1),
        name="gmlp",
    )(zu, zv, ln_g, ln_b, ws, bs_t)


def _dn_prep_kernel(x_ref, prev_ref, next_ref, cw_ref, qkv_ref, xe_ref):
    i = pl.program_id(1)
    tp = x_ref.shape[0]
    has_prev = (i > 0).astype(F32)
    has_next = (i < pl.num_programs(1) - 1).astype(F32)
    xe_ref[0:CONV_HALO, :] = prev_ref[...] * has_prev
    xe_ref[CONV_HALO:CONV_HALO + tp, :] = x_ref[...]
    xe_ref[CONV_HALO + tp:, :] = next_ref[...] * has_next
    for c in range(0, 3 * DN_WIDTH, HEAD_DIM):
        cols = slice(c, c + HEAD_DIM)
        acc = None
        for tap in range(CONV_WIDTH):
            start = CONV_HALO - CONV_WIDTH // 2 + tap
            term = xe_ref[start:start + tp, cols] * cw_ref[tap:tap + 1, cols]
            acc = term if acc is None else acc + term
        y = _silu(acc)
        if c < 2 * DN_WIDTH:
            y = y * lax.rsqrt(jnp.sum(y * y, axis=-1, keepdims=True) + EPS)
            if c < DN_WIDTH:
                y = y * (HEAD_DIM ** -0.5)
        qkv_ref[:, cols] = y.astype(qkv_ref.dtype)


def _dn_prep(zqkv, conv_w):
    b, s, w = zqkv.shape
    tp = PREP_TILE
    halo_blocks = tp // CONV_HALO
    last_halo = s // CONV_HALO - 1
    return pl.pallas_call(
        _dn_prep_kernel,
        grid=(b, s // tp),
        in_specs=[
            pl.BlockSpec((None, tp, w), lambda bi, i: (bi, i, 0)),
            pl.BlockSpec((None, CONV_HALO, w), lambda bi, i: (bi, jnp.maximum(i * halo_blocks - 1, 0), 0)),
            pl.BlockSpec((None, CONV_HALO, w), lambda bi, i: (bi, jnp.minimum((i + 1) * halo_blocks, last_halo), 0)),
            _resident(conv_w.shape),
        ],
        out_specs=pl.BlockSpec((None, tp, w), lambda bi, i: (bi, i, 0)),
        out_shape=jax.ShapeDtypeStruct((b, s, w), MXU_DTYPE),
        scratch_shapes=[pltpu.VMEM((tp + 2 * CONV_HALO, w), F32)],
        compiler_params=_params(2),
        name="dn_prep",
    )(zqkv, zqkv, zqkv, conv_w)


def _unit_triangular_inverse_offdiag(a, row, col):
    c = a.shape[0]
    shift = int(math.log2(INV_BASE))
    a_diag = jnp.where((row >> shift) == (col >> shift), a, 0.0)
    x = -a_diag
    power = a_diag
    for _ in range(shift - 1):
        power = _mm(power, power)
        x = x + power + _mm(x, power)
    while (1 << shift) < c:
        pair = ((row >> (shift + 1)) == (col >> (shift + 1))) & ((row >> shift) != (col >> shift))
        e = jnp.where(pair, a, 0.0)
        y = e + _mm(e, x)
        x = x - (y + _mm(x, y))
        shift += 1
    return x


def _dn_direction(qkv_ref, abc_ref, abr_ref, alog_c, dtb_c, alog_r, dtb_r, s_ref, o_ref, reverse):
    c = qkv_ref.shape[0]
    nh = DN_HEADS
    row = lax.broadcasted_iota(jnp.int32, (c, c), 0)
    col = lax.broadcasted_iota(jnp.int32, (c, c), 1)
    incl = (row <= col) if reverse else (row >= col)
    strict = (row < col) if reverse else (row > col)
    tri = incl.astype(MXU_DTYPE)

    ab_c = abc_ref[...]
    g_c = -jnp.exp(alog_c) * _softplus(ab_c[:, :2 * nh] + dtb_c)
    beta_c = jax.nn.sigmoid(ab_c[:, 2 * nh:])
    g_r = -jnp.exp(alog_r) * _softplus(abr_ref[0:2 * nh, :] + dtb_r)
    gc_c = sum(jnp.dot(tri, p, preferred_element_type=F32) for p in _split3(g_c))
    gc_r = sum(lax.dot_general(p, tri, (((1,), (1,)), ((), ())), preferred_element_type=F32)
               for p in _split3(g_r))
    end = 0 if reverse else c - 1
    gc_end = gc_c[end:end + 1, :]

    for h in range(nh):
        hd = (nh if reverse else 0) + h
        q = qkv_ref[:, h * HEAD_DIM:(h + 1) * HEAD_DIM].astype(F32)
        k = qkv_ref[:, DN_WIDTH + h * HEAD_DIM:DN_WIDTH + (h + 1) * HEAD_DIM].astype(F32)
        v = qkv_ref[:, 2 * DN_WIDTH + h * HEAD_DIM:2 * DN_WIDTH + (h + 1) * HEAD_DIM].astype(F32)
        gcc = gc_c[:, hd:hd + 1]
        gcr = gc_r[hd:hd + 1, :]
        beta = beta_c[:, hd:hd + 1]
        decay = jnp.exp(jnp.where(incl, gcc - gcr, -1e30))
        a = jnp.where(strict, beta * _mm_nt(k, k) * decay, 0.0)
        x = _unit_triangular_inverse_offdiag(a, row, col)
        eg = jnp.exp(gcc)
        rhs = jnp.concatenate([v * beta, k * (beta * eg)], axis=1)
        sol = rhs + _mm(x, rhs)
        u, w = sol[:, :HEAD_DIM], sol[:, HEAD_DIM:]
        state = s_ref[h]
        ws_qs = _mm(jnp.concatenate([w, q * eg], axis=0), state)
        v_new = u - ws_qs[:c]
        p = jnp.where(incl, _mm_nt(q, k) * decay, 0.0)
        o_ref[:, h * HEAD_DIM:(h + 1) * HEAD_DIM] = ws_qs[c:] + _mm(p, v_new)
        ge = gc_end[:, hd:hd + 1]
        s_ref[h] = state * jnp.exp(ge) + _mm_tn(k * jnp.exp(ge - gcc), v_new)


def _dn_scan_kernel(qkv_f_ref, qkv_b_ref, abc_f_ref, abc_b_ref, abr_f_ref, abr_b_ref,
                    alog_c_ref, dtb_c_ref, alog_r_ref, dtb_r_ref, of_ref, ob_ref, sf_ref, sb_ref):
    @pl.when(pl.program_id(1) == 0)
    def _():
        sf_ref[...] = jnp.zeros_like(sf_ref)
        sb_ref[...] = jnp.zeros_like(sb_ref)

    gate = (alog_c_ref[...], dtb_c_ref[...], alog_r_ref[...], dtb_r_ref[...])
    _dn_direction(qkv_f_ref, abc_f_ref, abr_f_ref, *gate, sf_ref, of_ref, reverse=False)
    _dn_direction(qkv_b_ref, abc_b_ref, abr_b_ref, *gate, sb_ref, ob_ref, reverse=True)


def _dn_scan(qkv, zab, zab_t, a_log, dt_bias):
    b, s, w = qkv.shape
    c = DN_CHUNK
    n = s // c
    n_ab = zab.shape[-1]
    fwd = lambda bi, t: (bi, t, 0)
    bwd = lambda bi, t: (bi, n - 1 - t, 0)
    fwd_t = lambda bi, t: (bi, 0, t)
    bwd_t = lambda bi, t: (bi, 0, n - 1 - t)
    alog_c, dtb_c = a_log.reshape(1, -1), dt_bias.reshape(1, -1)
    alog_r, dtb_r = a_log.reshape(-1, 1), dt_bias.reshape(-1, 1)
    o_shape = jax.ShapeDtypeStruct((b, s, DN_WIDTH), F32)
    return pl.pallas_call(
        _dn_scan_kernel,
        grid=(b, n),
        in_specs=[
            pl.BlockSpec((None, c, w), fwd), pl.BlockSpec((None, c, w), bwd),
            pl.BlockSpec((None, c, n_ab), fwd), pl.BlockSpec((None, c, n_ab), bwd),
            pl.BlockSpec((None, n_ab, c), fwd_t), pl.BlockSpec((None, n_ab, c), bwd_t),
            _resident(alog_c.shape), _resident(dtb_c.shape), _resident(alog_r.shape), _resident(dtb_r.shape),
        ],
        out_specs=[pl.BlockSpec((None, c, DN_WIDTH), fwd), pl.BlockSpec((None, c, DN_WIDTH), bwd)],
        out_shape=[o_shape, o_shape],
        scratch_shapes=[pltpu.VMEM((DN_HEADS, HEAD_DIM, HEAD_DIM), F32),
                        pltpu.VMEM((DN_HEADS, HEAD_DIM, HEAD_DIM), F32)],
        compiler_params=_params(2),
        name="dn_scan",
    )(qkv, qkv, zab, zab, zab_t, zab_t, alog_c, dtb_c, alog_r, dtb_r)


def _tail_kernel(h_ref, ygm_ref, of_ref, ob_ref, zg_ref, p_ref, on_ref, wo_ref, n2_ref, wg_ref, wu_ref, wd_ref,
                 pp_ref, pn_ref, gn_ref, gw_ref, gb_ref, fn_ref, out_ref):
    o = of_ref[...] + ob_ref[...]
    parts = [ygm_ref[...]]
    for hd in range(DN_HEADS):
        cols = slice(hd * HEAD_DIM, (hd + 1) * HEAD_DIM)
        y = _rms(o[:, cols], on_ref[...]) * _silu(zg_ref[:, cols])
        parts.append(y.astype(MXU_DTYPE))
    h = h_ref[...] + _mm(jnp.concatenate(parts, axis=1), wo_ref[...])
    h = h + 0.5 * _swiglu(_rms(h, n2_ref[...]).astype(MXU_DTYPE), wg_ref, wu_ref, wd_ref)
    e = _rms(_mm(p_ref[...], pp_ref[...]), pn_ref[...])
    gate = jax.nn.sigmoid(_mm(_rms(h, gn_ref[...]), gw_ref[...]) + gb_ref[...])
    out_ref[...] = _rms(h + gate * e, fn_ref[...])


def _tail(h1, y_gm, o_f, o_b, zg, p, *weights):
    t, d = h1.shape
    tm = TOKEN_TILE
    row = lambda w: pl.BlockSpec((tm, w), lambda i: (i, 0))
    return pl.pallas_call(
        _tail_kernel,
        grid=(t // tm,),
        in_specs=[row(d), row(GM_WIDTH), row(DN_WIDTH), row(DN_WIDTH), row(DN_WIDTH), row(p.shape[1])]
                 + [_resident(w.shape) for w in weights],
        out_specs=row(d),
        out_shape=jax.ShapeDtypeStruct((t, d), F32),
        compiler_params=_params(1),
        name="tail",
    )(h1, y_gm, o_f, o_b, zg, p, *weights)


def _layer(h, p, ffn1_norm, ffn1_w_gate, ffn1_w_up, ffn1_w_down, mix_norm, w_in, gm_ln_g, gm_ln_b, gm_ws, gm_bs,
           dn_conv_w, dn_a_log, dn_dt_bias, dn_out_norm, w_out, ffn2_norm, ffn2_w_gate, ffn2_w_up, ffn2_w_down,
           ple_proj, ple_norm, ple_gate_norm, ple_gate_w, ple_gate_b, out_norm, bsz, seq):
    wdt = lambda w: w.astype(MXU_DTYPE)
    vec = lambda g: g.reshape(1, -1)
    n_main = 2 * GM_WIDTH + 4 * DN_WIDTH
    h1, zu, zv, zqkv, zg, zab = _ffn1_mix(h, vec(ffn1_norm), wdt(ffn1_w_gate), wdt(ffn1_w_up), wdt(ffn1_w_down),
                                          vec(mix_norm), wdt(w_in[:, :n_main]), wdt(w_in[:, n_main:]))
    y_gm = _gmlp(zu, zv, gm_ln_g, gm_ln_b, wdt(gm_ws), gm_bs.T)
    qkv = _dn_prep(zqkv.reshape(bsz, seq, -1), dn_conv_w)
    zab = zab.reshape(bsz, seq, -1)
    o_f, o_b = _dn_scan(qkv, zab, jnp.swapaxes(zab, 1, 2), dn_a_log, dn_dt_bias)
    return _tail(h1, y_gm, o_f.reshape(-1, DN_WIDTH), o_b.reshape(-1, DN_WIDTH), zg, p,
                 vec(dn_out_norm), wdt(w_out), vec(ffn2_norm), wdt(ffn2_w_gate), wdt(ffn2_w_up), wdt(ffn2_w_down),
                 wdt(ple_proj), vec(ple_norm), vec(ple_gate_norm), wdt(ple_gate_w), vec(ple_gate_b), vec(out_norm))


def kernel(x, p, ffn1_norm, ffn1_w_gate, ffn1_w_up, ffn1_w_down, mix_norm, w_in, gm_ln_g, gm_ln_b, gm_ws, gm_bs, dn_conv_w, dn_a_log, dn_dt_bias, dn_out_norm, w_out, ffn2_norm, ffn2_w_gate, ffn2_w_up, ffn2_w_down, ple_proj, ple_norm, ple_gate_norm, ple_gate_w, ple_gate_b, final_norm):
    bsz, seq, d = x.shape
    depth = p.shape[0]
    assert depth == 1, "the final rmsnorm is fused into the layer kernel, so exactly one layer is supported"
    per_layer = (ffn1_norm, ffn1_w_gate, ffn1_w_up, ffn1_w_down, mix_norm, w_in, gm_ln_g, gm_ln_b, gm_ws, gm_bs,
                 dn_conv_w, dn_a_log, dn_dt_bias, dn_out_norm, w_out, ffn2_norm, ffn2_w_gate, ffn2_w_up,
                 ffn2_w_down, ple_proj, ple_norm, ple_gate_norm, ple_gate_w, ple_gate_b)
    out = _layer(x.reshape(bsz * seq, d), p[0].reshape(bsz * seq, -1), *(w[0] for w in per_layer),
                 final_norm, bsz, seq)
    return out.reshape(bsz, seq, d)
```

```python
import math

import jax
import jax.numpy as jnp
from jax import lax
from jax.experimental import pallas as pl
from jax.experimental.pallas import tpu as pltpu

F32 = jnp.float32
MXU_DTYPE = jnp.bfloat16
EPS = 1e-6

HEAD_DIM = 128
GM_HEADS = 4
DN_HEADS = 4
GM_WIDTH = GM_HEADS * HEAD_DIM
DN_WIDTH = DN_HEADS * HEAD_DIM
GM_CHUNK = 128
DN_CHUNK = 128
INV_BASE = 16
CONV_WIDTH = 5
CONV_HALO = 8
VMEM_LIMIT_BYTES = 56 * 1024 * 1024

TOKEN_TILE = 512
FF_TILE = 256
PREP_TILE = 512
INTRA_CHUNKS = 2
BLOB_U, BLOB_W, BLOB_QD, BLOB_KDT, BLOB_P = range(5)
BLOB_GROUPS = 5


def _mm(a, b):
    return jnp.dot(a.astype(MXU_DTYPE), b.astype(MXU_DTYPE), preferred_element_type=F32)


def _split3(x):
    hi = x.astype(MXU_DTYPE)
    r1 = x - hi.astype(F32)
    mid = r1.astype(MXU_DTYPE)
    lo = (r1 - mid.astype(F32)).astype(MXU_DTYPE)
    return hi, mid, lo


def _rms(x, g):
    return x * lax.rsqrt(jnp.mean(x * x, axis=-1, keepdims=True) + EPS) * g


def _silu(x):
    return x * jax.nn.sigmoid(x)


def _gelu_tanh(x):
    return 0.5 * x * (1.0 + jnp.tanh(math.sqrt(2.0 / math.pi) * (x + 0.044715 * (x * x * x))))


def _softplus(x):
    return jnp.maximum(x, 0.0) + jnp.log1p(jnp.exp(-jnp.abs(x)))


def _swiglu(xn, wg_ref, wu_ref, wd_ref):
    d_ff = wg_ref.shape[1]
    hidden = []
    for j in range(0, d_ff, FF_TILE):
        g = _mm(xn, wg_ref[:, j:j + FF_TILE])
        u = _mm(xn, wu_ref[:, j:j + FF_TILE])
        hidden.append((_silu(g) * u).astype(MXU_DTYPE))
    return _mm(jnp.concatenate(hidden, axis=1), wd_ref[...])


def _resident(shape):
    nd = len(shape)
    return pl.BlockSpec(shape, lambda *_: (0,) * nd, pipeline_mode=pl.Buffered(1))


def _params(n_axes):
    return pltpu.CompilerParams(dimension_semantics=("arbitrary",) * n_axes, vmem_limit_bytes=VMEM_LIMIT_BYTES)


def _gmlp_chunk(zu, zv, lng, lnb, ws, bias):
    v = _gelu_tanh(zv)
    xc = v - jnp.mean(v, axis=-1, keepdims=True)
    vn = xc * lax.rsqrt(jnp.mean(xc * xc, axis=-1, keepdims=True) + EPS) * lng + lnb
    return _gelu_tanh(zu) * (_mm(ws, vn) + bias)


def _ffn1_mix_kernel(x_ref, n1_ref, wg_ref, wu_ref, wd_ref, mn_ref, win_ref, wab_ref,
                     lng_ref, lnb_ref, ws_ref, bst_ref,
                     h_ref, ygm_ref, zqkv_ref, zg_ref, zab_ref):
    x = x_ref[...]
    xn = _rms(x, n1_ref[...]).astype(MXU_DTYPE)
    h = x + 0.5 * _swiglu(xn, wg_ref, wu_ref, wd_ref)
    h_ref[...] = h
    n = _rms(h, mn_ref[...]).astype(MXU_DTYPE)
    zu = _mm(n, win_ref[:, 0:GM_WIDTH])
    zv = _mm(n, win_ref[:, GM_WIDTH:2 * GM_WIDTH])
    c0 = 2 * GM_WIDTH
    for ref in (zqkv_ref, zg_ref):
        width = ref.shape[1]
        for j in range(0, width, 512):
            ref[:, j:j + 512] = _mm(n, win_ref[:, c0 + j:c0 + j + 512]).astype(ref.dtype)
        c0 += width
    zab_ref[...] = _mm(n, wab_ref[...])
    for c in range(0, x.shape[0], GM_CHUNK):
        for hd in range(GM_HEADS):
            cols = slice(hd * HEAD_DIM, (hd + 1) * HEAD_DIM)
            y = _gmlp_chunk(zu[c:c + GM_CHUNK, cols], zv[c:c + GM_CHUNK, cols], lng_ref[hd:hd + 1, :],
                            lnb_ref[hd:hd + 1, :], ws_ref[hd], bst_ref[:, hd:hd + 1])
            ygm_ref[c:c + GM_CHUNK, cols] = y.astype(ygm_ref.dtype)


def _ffn1_mix(x, *weights):
    t, d = x.shape
    tm = TOKEN_TILE
    n_ab = weights[6].shape[1]
    row = lambda w: pl.BlockSpec((tm, w), lambda i: (i, 0))
    outs = [(d, F32), (GM_WIDTH, MXU_DTYPE), (3 * DN_WIDTH, F32), (DN_WIDTH, F32), (n_ab, F32)]
    return pl.pallas_call(
        _ffn1_mix_kernel,
        grid=(t // tm,),
        in_specs=[row(d)] + [_resident(w.shape) for w in weights],
        out_specs=[row(w) for w, _ in outs],
        out_shape=[jax.ShapeDtypeStruct((t, w), dt) for w, dt in outs],
        compiler_params=_params(1),
        name="ffn1_mix",
    )(x, *weights)


def _dn_prep_kernel(x_ref, prev_ref, next_ref, cw_ref, qkv_ref, xe_ref):
    i = pl.program_id(1)
    tp = x_ref.shape[0]
    has_prev = (i > 0).astype(F32)
    has_next = (i < pl.num_programs(1) - 1).astype(F32)
    xe_ref[0:CONV_HALO, :] = prev_ref[...] * has_prev
    xe_ref[CONV_HALO:CONV_HALO + tp, :] = x_ref[...]
    xe_ref[CONV_HALO + tp:, :] = next_ref[...] * has_next
    for c in range(0, 3 * DN_WIDTH, HEAD_DIM):
        cols = slice(c, c + HEAD_DIM)
        acc = None
        for tap in range(CONV_WIDTH):
            start = CONV_HALO - CONV_WIDTH // 2 + tap
            term = xe_ref[start:start + tp, cols] * cw_ref[tap:tap + 1, cols]
            acc = term if acc is None else acc + term
        y = _silu(acc)
        if c < 2 * DN_WIDTH:
            y = y * lax.rsqrt(jnp.sum(y * y, axis=-1, keepdims=True) + EPS)
            if c < DN_WIDTH:
                y = y * (HEAD_DIM ** -0.5)
        qkv_ref[:, cols] = y.astype(qkv_ref.dtype)


def _dn_prep(zqkv, conv_w):
    b, s, w = zqkv.shape
    tp = PREP_TILE
    halo_blocks = tp // CONV_HALO
    last_halo = s // CONV_HALO - 1
    return pl.pallas_call(
        _dn_prep_kernel,
        grid=(b, s // tp),
        in_specs=[
            pl.BlockSpec((None, tp, w), lambda bi, i: (bi, i, 0)),
            pl.BlockSpec((None, CONV_HALO, w), lambda bi, i: (bi, jnp.maximum(i * halo_blocks - 1, 0), 0)),
            pl.BlockSpec((None, CONV_HALO, w), lambda bi, i: (bi, jnp.minimum((i + 1) * halo_blocks, last_halo), 0)),
            _resident(conv_w.shape),
        ],
        out_specs=pl.BlockSpec((None, tp, w), lambda bi, i: (bi, i, 0)),
        out_shape=jax.ShapeDtypeStruct((b, s, w), MXU_DTYPE),
        scratch_shapes=[pltpu.VMEM((tp + 2 * CONV_HALO, w), F32)],
        compiler_params=_params(2),
        name="dn_prep",
    )(zqkv, zqkv, zqkv, conv_w)


def _unit_triangular_inverses_offdiag(a_list, row, col):
    c = a_list[0].shape[0]
    shift = int(math.log2(INV_BASE))
    diag = (row >> shift) == (col >> shift)
    powers = [jnp.where(diag, a, 0.0) for a in a_list]
    xs = [-p for p in powers]
    for _ in range(shift - 1):
        powers = [_mm(p, p) for p in powers]
        xs = [x + p + _mm(x, p) for x, p in zip(xs, powers)]
    while (1 << shift) < c:
        pair = ((row >> (shift + 1)) == (col >> (shift + 1))) & ((row >> shift) != (col >> shift))
        es = [jnp.where(pair, a, 0.0) for a in a_list]
        ys = [e + _mm(e, x) for e, x in zip(es, xs)]
        xs = [x - (y + _mm(x, y)) for x, y in zip(xs, ys)]
        shift += 1
    return xs


def _dn_intra_kernel(qkv_ref, abc_ref, abr_ref, alog_c_ref, dtb_c_ref, alog_r_ref, dtb_r_ref,
                     blob_f_ref, blob_b_ref, gl_ref):
    c, nh = DN_CHUNK, DN_HEADS
    n_chunks = qkv_ref.shape[0] // c
    row = lax.broadcasted_iota(jnp.int32, (c, c), 0)
    col = lax.broadcasted_iota(jnp.int32, (c, c), 1)
    incl = {False: row >= col, True: row <= col}
    strict = {False: row > col, True: row < col}
    tri = incl[False].astype(MXU_DTYPE)
    fwd_lanes = lax.broadcasted_iota(jnp.int32, (c, 2 * nh), 1) < nh
    fwd_rows = lax.broadcasted_iota(jnp.int32, (2 * nh, c), 0) < nh
    blobs = {False: blob_f_ref, True: blob_b_ref}

    ab_c = abc_ref[...]
    g_c_all = -jnp.exp(alog_c_ref[...]) * _softplus(ab_c[:, :2 * nh] + dtb_c_ref[...])
    beta_all = jax.nn.sigmoid(ab_c[:, 2 * nh:])
    g_r_all = -jnp.exp(alog_r_ref[...]) * _softplus(abr_ref[0:2 * nh, :] + dtb_r_ref[...])

    chains, gates = [], []
    for ci in range(n_chunks):
        rows = slice(ci * c, (ci + 1) * c)
        g_c, g_r = g_c_all[rows], g_r_all[:, rows]
        pre_c = sum(jnp.dot(tri, part, preferred_element_type=F32) for part in _split3(g_c))
        pre_r = sum(lax.dot_general(part, tri, (((1,), (1,)), ((), ())), preferred_element_type=F32)
                    for part in _split3(g_r))
        tot_c, tot_r = pre_c[c - 1:c, :], pre_r[:, c - 1:c]
        gc_c = jnp.where(fwd_lanes, pre_c, tot_c - pre_c + g_c)
        gc_r = jnp.where(fwd_rows, pre_r, tot_r - pre_r + g_r)
        gl_ref[ci] = jnp.exp(tot_c)
        gates.append((rows, gc_c, gc_r, tot_c))
    heads = []
    for rows, gc_c, gc_r, tot_c in gates:
        for h in range(nh):
            q = qkv_ref[rows, h * HEAD_DIM:(h + 1) * HEAD_DIM].astype(F32)
            k = qkv_ref[rows, DN_WIDTH + h * HEAD_DIM:DN_WIDTH + (h + 1) * HEAD_DIM].astype(F32)
            v = qkv_ref[rows, 2 * DN_WIDTH + h * HEAD_DIM:2 * DN_WIDTH + (h + 1) * HEAD_DIM].astype(F32)
            heads.append(dict(rows=rows, h=h, q=q, k=k, v=v, kt=k.T, gc_c=gc_c, gc_r=gc_r, tot_c=tot_c))
    qk_kk = [_mm(jnp.concatenate([hd["q"], hd["k"]], axis=0), hd["kt"]) for hd in heads]
    for hd, prod in zip(heads, qk_kk):
        for reverse in (False, True):
            lane = (nh if reverse else 0) + hd["h"]
            chains.append(dict(rows=hd["rows"], h=hd["h"], reverse=reverse, q=hd["q"], k=hd["k"], v=hd["v"],
                               kt=hd["kt"], qk=prod[:c], kk=prod[c:],
                               gcc=hd["gc_c"][:, lane:lane + 1], gcr=hd["gc_r"][lane:lane + 1, :],
                               beta=beta_all[hd["rows"], lane:lane + 1], ge=hd["tot_c"][:, lane:lane + 1]))

    def store(ch, group, value):
        start = group * DN_WIDTH + ch["h"] * HEAD_DIM
        blobs[ch["reverse"]][ch["rows"], start:start + HEAD_DIM] = value.astype(MXU_DTYPE)

    a_list = []
    for ch in chains:
        decay = jnp.exp(jnp.where(incl[ch["reverse"]], ch["gcc"] - ch["gcr"], -1e30))
        a_list.append(jnp.where(strict[ch["reverse"]], ch["beta"] * ch["kk"] * decay, 0.0))
        store(ch, BLOB_P, jnp.where(incl[ch["reverse"]], ch["qk"] * decay, 0.0))
    xs = _unit_triangular_inverses_offdiag(a_list, row, col)
    for ch, x in zip(chains, xs):
        eg = jnp.exp(ch["gcc"])
        rhs = jnp.concatenate([ch["v"] * ch["beta"], ch["k"] * (ch["beta"] * eg)], axis=1)
        sol = rhs + _mm(x, rhs)
        store(ch, BLOB_U, sol[:, :HEAD_DIM])
        store(ch, BLOB_W, sol[:, HEAD_DIM:])
        store(ch, BLOB_QD, ch["q"] * eg)
        store(ch, BLOB_KDT, ch["kt"] * jnp.exp(ch["ge"] - ch["gcr"]))


def _dn_intra(qkv, zab, zab_t, a_log, dt_bias):
    b, s, w = qkv.shape
    rows = INTRA_CHUNKS * DN_CHUNK
    n_ab = zab.shape[-1]
    alog_c, dtb_c = a_log.reshape(1, -1), dt_bias.reshape(1, -1)
    alog_r, dtb_r = a_log.reshape(-1, 1), dt_bias.reshape(-1, 1)
    blob = jax.ShapeDtypeStruct((b, s, BLOB_GROUPS * DN_WIDTH), MXU_DTYPE)
    blob_spec = pl.BlockSpec((None, rows, BLOB_GROUPS * DN_WIDTH), lambda bi, i: (bi, i, 0))
    return pl.pallas_call(
        _dn_intra_kernel,
        grid=(b, s // rows),
        in_specs=[
            pl.BlockSpec((None, rows, w), lambda bi, i: (bi, i, 0)),
            pl.BlockSpec((None, rows, n_ab), lambda bi, i: (bi, i, 0)),
            pl.BlockSpec((None, n_ab, rows), lambda bi, i: (bi, 0, i)),
            _resident(alog_c.shape), _resident(dtb_c.shape), _resident(alog_r.shape), _resident(dtb_r.shape),
        ],
        out_specs=[blob_spec, blob_spec,
                   pl.BlockSpec((None, INTRA_CHUNKS, 1, 2 * DN_HEADS), lambda bi, i: (bi, i, 0, 0))],
        out_shape=[blob, blob, jax.ShapeDtypeStruct((b, s // DN_CHUNK, 1, 2 * DN_HEADS), F32)],
        compiler_params=_params(2),
        name="dn_intra",
    )(qkv, zab, zab_t, alog_c, dtb_c, alog_r, dtb_r)


def _dn_scan_kernel(gl_ref, blob_f_ref, blob_b_ref, of_ref, ob_ref, s_ref):
    bi, t, n = pl.program_id(0), pl.program_id(1), pl.num_programs(1)
    c, nh = DN_CHUNK, DN_HEADS

    @pl.when(t == 0)
    def _():
        s_ref[...] = jnp.zeros_like(s_ref)

    chains = [(blob_f_ref, of_ref, t, h, h) for h in range(nh)]
    chains += [(blob_b_ref, ob_ref, n - 1 - t, h, nh + h) for h in range(nh)]

    def group(blob_ref, g, h):
        start = g * DN_WIDTH + h * HEAD_DIM
        return blob_ref[:, start:start + HEAD_DIM]

    def stacked(blob_ref, g0, g1, h):
        return jnp.concatenate([group(blob_ref, g0, h), group(blob_ref, g1, h)], axis=0)

    ws_qs = [_mm(stacked(blob, BLOB_W, BLOB_QD, h), s_ref[hd]) for blob, _, _, h, hd in chains]
    v_news = [group(blob, BLOB_U, h).astype(F32) - r[:c] for (blob, _, _, h, _), r in zip(chains, ws_qs)]
    pv_kv = [_mm(stacked(blob, BLOB_P, BLOB_KDT, h), v_new) for (blob, _, _, h, _), v_new in zip(chains, v_news)]
    for (_, o_ref, chunk, h, hd), r, pk in zip(chains, ws_qs, pv_kv):
        o_ref[:, h * HEAD_DIM:(h + 1) * HEAD_DIM] = r[c:] + pk[:c]
        s_ref[hd] = s_ref[hd] * gl_ref[(bi * n + chunk) * (2 * nh) + hd] + pk[c:]


def _dn_scan(gl_flat, blob_f, blob_b):
    b, s, w = blob_f.shape
    c = DN_CHUNK
    n = s // c
    fwd = lambda bi, t: (bi, t, 0)
    bwd = lambda bi, t: (bi, n - 1 - t, 0)
    o_shape = jax.ShapeDtypeStruct((b, s, DN_WIDTH), F32)
    return pl.pallas_call(
        _dn_scan_kernel,
        grid=(b, n),
        in_specs=[pl.BlockSpec(memory_space=pltpu.SMEM),
                  pl.BlockSpec((None, c, w), fwd), pl.BlockSpec((None, c, w), bwd)],
        out_specs=[pl.BlockSpec((None, c, DN_WIDTH), fwd), pl.BlockSpec((None, c, DN_WIDTH), bwd)],
        out_shape=[o_shape, o_shape],
        scratch_shapes=[pltpu.VMEM((2 * DN_HEADS, HEAD_DIM, HEAD_DIM), F32)],
        compiler_params=_params(2),
        name="dn_scan",
    )(gl_flat, blob_f, blob_b)


def _tail_kernel(h_ref, ygm_ref, of_ref, ob_ref, zg_ref, p_ref, on_ref, wo_ref, n2_ref, wg_ref, wu_ref, wd_ref,
                 pp_ref, pn_ref, gn_ref, gw_ref, gb_ref, fn_ref, out_ref):
    o = of_ref[...] + ob_ref[...]
    parts = [ygm_ref[...]]
    for hd in range(DN_HEADS):
        cols = slice(hd * HEAD_DIM, (hd + 1) * HEAD_DIM)
        y = _rms(o[:, cols], on_ref[...]) * _silu(zg_ref[:, cols])
        parts.append(y.astype(MXU_DTYPE))
    h = h_ref[...] + _mm(jnp.concatenate(parts, axis=1), wo_ref[...])
    h = h + 0.5 * _swiglu(_rms(h, n2_ref[...]).astype(MXU_DTYPE), wg_ref, wu_ref, wd_ref)
    e = _rms(_mm(p_ref[...], pp_ref[...]), pn_ref[...])
    gate = jax.nn.sigmoid(_mm(_rms(h, gn_ref[...]), gw_ref[...]) + gb_ref[...])
    out_ref[...] = _rms(h + gate * e, fn_ref[...])


def _tail(h1, y_gm, o_f, o_b, zg, p, *weights):
    t, d = h1.shape
    tm = TOKEN_TILE
    row = lambda w: pl.BlockSpec((tm, w), lambda i: (i, 0))
    return pl.pallas_call(
        _tail_kernel,
        grid=(t // tm,),
        in_specs=[row(d), row(GM_WIDTH), row(DN_WIDTH), row(DN_WIDTH), row(DN_WIDTH), row(p.shape[1])]
                 + [_resident(w.shape) for w in weights],
        out_specs=row(d),
        out_shape=jax.ShapeDtypeStruct((t, d), F32),
        compiler_params=_params(1),
        name="tail",
    )(h1, y_gm, o_f, o_b, zg, p, *weights)


def _layer(h, p, ffn1_norm, ffn1_w_gate, ffn1_w_up, ffn1_w_down, mix_norm, w_in, gm_ln_g, gm_ln_b, gm_ws, gm_bs,
           dn_conv_w, dn_a_log, dn_dt_bias, dn_out_norm, w_out, ffn2_norm, ffn2_w_gate, ffn2_w_up, ffn2_w_down,
           ple_proj, ple_norm, ple_gate_norm, ple_gate_w, ple_gate_b, out_norm, bsz, seq):
    wdt = lambda w: w.astype(MXU_DTYPE)
    vec = lambda g: g.reshape(1, -1)
    n_main = 2 * GM_WIDTH + 4 * DN_WIDTH
    h1, y_gm, zqkv, zg, zab = _ffn1_mix(h, vec(ffn1_norm), wdt(ffn1_w_gate), wdt(ffn1_w_up), wdt(ffn1_w_down),
                                        vec(mix_norm), wdt(w_in[:, :n_main]), wdt(w_in[:, n_main:]),
                                        gm_ln_g, gm_ln_b, wdt(gm_ws), gm_bs.T)
    qkv = _dn_prep(zqkv.reshape(bsz, seq, -1), dn_conv_w)
    zab = zab.reshape(bsz, seq, -1)
    blob_f, blob_b, gl = _dn_intra(qkv, zab, jnp.swapaxes(zab, 1, 2), dn_a_log, dn_dt_bias)
    o_f, o_b = _dn_scan(gl.reshape(-1), blob_f, blob_b)
    return _tail(h1, y_gm, o_f.reshape(-1, DN_WIDTH), o_b.reshape(-1, DN_WIDTH), zg, p,
                 vec(dn_out_norm), wdt(w_out), vec(ffn2_norm), wdt(ffn2_w_gate), wdt(ffn2_w_up), wdt(ffn2_w_down),
                 wdt(ple_proj), vec(ple_norm), vec(ple_gate_norm), wdt(ple_gate_w), vec(ple_gate_b), vec(out_norm))


def kernel(x, p, ffn1_norm, ffn1_w_gate, ffn1_w_up, ffn1_w_down, mix_norm, w_in, gm_ln_g, gm_ln_b, gm_ws, gm_bs, dn_conv_w, dn_a_log, dn_dt_bias, dn_out_norm, w_out, ffn2_norm, ffn2_w_gate, ffn2_w_up, ffn2_w_down, ple_proj, ple_norm, ple_gate_norm, ple_gate_w, ple_gate_b, final_norm):
    bsz, seq, d = x.shape
    depth = p.shape[0]
    assert depth == 1, "the final rmsnorm is fused into the layer kernel, so exactly one layer is supported"
    per_layer = (ffn1_norm, ffn1_w_gate, ffn1_w_up, ffn1_w_down, mix_norm, w_in, gm_ln_g, gm_ln_b, gm_ws, gm_bs,
                 dn_conv_w, dn_a_log, dn_dt_bias, dn_out_norm, w_out, ffn2_norm, ffn2_w_gate, ffn2_w_up,
                 ffn2_w_down, ple_proj, ple_norm, ple_gate_norm, ple_gate_w, ple_gate_b)
    out = _layer(x.reshape(bsz * seq, d), p[0].reshape(bsz * seq, -1), *(w[0] for w in per_layer),
                 final_norm, bsz, seq)
    return out.reshape(bsz, seq, d)
```

```python
import math

import jax
import jax.numpy as jnp
from jax import lax
from jax.experimental import pallas as pl
from jax.experimental.pallas import tpu as pltpu

F32 = jnp.float32
MXU_DTYPE = jnp.bfloat16
EPS = 1e-6

HEAD_DIM = 128
GM_HEADS = 4
DN_HEADS = 4
GM_WIDTH = GM_HEADS * HEAD_DIM
DN_WIDTH = DN_HEADS * HEAD_DIM
GM_CHUNK = 128
DN_CHUNK = 128
INV_BASE = 16
CONV_WIDTH = 5
CONV_HALO = 16
VMEM_LIMIT_BYTES = 56 * 1024 * 1024

TOKEN_TILE = 512
FF_TILE = 256
PREP_TILE = 512
PREP_BLOCK = 128
PREP_COLS = 256
INTRA_CHUNKS = 2
BLOB_U, BLOB_W, BLOB_QD, BLOB_KDT, BLOB_P = range(5)
BLOB_GROUPS = 5
SCAN_ROWS = 4


def _mm(a, b):
    return jnp.dot(a.astype(MXU_DTYPE), b.astype(MXU_DTYPE), preferred_element_type=F32)


def _split3(x):
    hi = x.astype(MXU_DTYPE)
    r1 = x - hi.astype(F32)
    mid = r1.astype(MXU_DTYPE)
    lo = (r1 - mid.astype(F32)).astype(MXU_DTYPE)
    return hi, mid, lo


def _rms(x, g):
    return x * lax.rsqrt(jnp.mean(x * x, axis=-1, keepdims=True) + EPS) * g


def _silu(x):
    return x * jax.nn.sigmoid(x)


def _gelu_tanh(x):
    return 0.5 * x * (1.0 + jnp.tanh(math.sqrt(2.0 / math.pi) * (x + 0.044715 * (x * x * x))))


def _softplus(x):
    return jnp.maximum(x, 0.0) + jnp.log1p(jnp.exp(-jnp.abs(x)))


def _swiglu(xn, wg_ref, wu_ref, wd_ref):
    d_ff = wg_ref.shape[1]
    hidden = []
    for j in range(0, d_ff, FF_TILE):
        g = _mm(xn, wg_ref[:, j:j + FF_TILE])
        u = _mm(xn, wu_ref[:, j:j + FF_TILE])
        hidden.append((_silu(g) * u).astype(MXU_DTYPE))
    return _mm(jnp.concatenate(hidden, axis=1), wd_ref[...])


def _resident(shape):
    nd = len(shape)
    return pl.BlockSpec(shape, lambda *_: (0,) * nd, pipeline_mode=pl.Buffered(1))


def _params(n_axes):
    return pltpu.CompilerParams(dimension_semantics=("arbitrary",) * n_axes, vmem_limit_bytes=VMEM_LIMIT_BYTES)


def _gmlp_chunk(zu, zv, lng, lnb, ws, bias):
    v = _gelu_tanh(zv)
    xc = v - jnp.mean(v, axis=-1, keepdims=True)
    vn = xc * lax.rsqrt(jnp.mean(xc * xc, axis=-1, keepdims=True) + EPS) * lng + lnb
    return _gelu_tanh(zu) * (_mm(ws, vn) + bias)


def _ffn1_mix_kernel(x_ref, n1_ref, wg_ref, wu_ref, wd_ref, mn_ref, win_ref, wab_ref,
                     lng_ref, lnb_ref, ws_ref, bst_ref,
                     h_ref, ygm_ref, zqkv_ref, zg_ref, zab_ref):
    x = x_ref[...]
    xn = _rms(x, n1_ref[...]).astype(MXU_DTYPE)
    h = x + 0.5 * _swiglu(xn, wg_ref, wu_ref, wd_ref)
    h_ref[...] = h
    n = _rms(h, mn_ref[...]).astype(MXU_DTYPE)
    zu = _mm(n, win_ref[:, 0:GM_WIDTH])
    zv = _mm(n, win_ref[:, GM_WIDTH:2 * GM_WIDTH])
    c0 = 2 * GM_WIDTH
    for ref in (zqkv_ref, zg_ref):
        width = ref.shape[1]
        for j in range(0, width, 512):
            ref[:, j:j + 512] = _mm(n, win_ref[:, c0 + j:c0 + j + 512]).astype(ref.dtype)
        c0 += width
    zab_ref[...] = _mm(n, wab_ref[...])
    for c in range(0, x.shape[0], GM_CHUNK):
        for hd in range(GM_HEADS):
            cols = slice(hd * HEAD_DIM, (hd + 1) * HEAD_DIM)
            y = _gmlp_chunk(zu[c:c + GM_CHUNK, cols], zv[c:c + GM_CHUNK, cols], lng_ref[hd:hd + 1, :],
                            lnb_ref[hd:hd + 1, :], ws_ref[hd], bst_ref[:, hd:hd + 1])
            ygm_ref[c:c + GM_CHUNK, cols] = y.astype(ygm_ref.dtype)


def _ffn1_mix(x, *weights):
    t, d = x.shape
    tm = TOKEN_TILE
    n_ab = weights[6].shape[1]
    row = lambda w: pl.BlockSpec((tm, w), lambda i: (i, 0))
    outs = [(d, F32), (GM_WIDTH, MXU_DTYPE), (3 * DN_WIDTH, MXU_DTYPE), (DN_WIDTH, MXU_DTYPE), (n_ab, F32)]
    return pl.pallas_call(
        _ffn1_mix_kernel,
        grid=(t // tm,),
        in_specs=[row(d)] + [_resident(w.shape) for w in weights],
        out_specs=[row(w) for w, _ in outs],
        out_shape=[jax.ShapeDtypeStruct((t, w), dt) for w, dt in outs],
        compiler_params=_params(1),
        name="ffn1_mix",
    )(x, *weights)


def _dn_prep_kernel(x_ref, prev_ref, next_ref, cw_ref, qkv_ref, xe_ref):
    i = pl.program_id(1)
    tp, halo, blk = x_ref.shape[0], CONV_HALO, PREP_BLOCK
    xe_ref[0:halo, :] = jnp.where(i > 0, prev_ref[...], jnp.zeros_like(prev_ref))
    xe_ref[halo:halo + tp, :] = x_ref[...]
    xe_ref[halo + tp:, :] = jnp.where(i < pl.num_programs(1) - 1, next_ref[...], jnp.zeros_like(next_ref))
    centre = CONV_WIDTH // 2
    side_taps = [tap for tap in range(CONV_WIDTH) if tap != centre]
    out_row = lax.broadcasted_iota(jnp.int32, (blk, blk + 2 * halo), 0)
    in_row = lax.broadcasted_iota(jnp.int32, (blk, blk + 2 * halo), 1)
    shift = jnp.concatenate([(in_row == out_row + (halo + tap - centre)).astype(MXU_DTYPE) for tap in side_taps],
                            axis=0)
    for c in range(0, 3 * DN_WIDTH, PREP_COLS):
        cols = slice(c, c + PREP_COLS)
        for r in range(0, tp, blk):
            shifted = jnp.dot(shift, xe_ref[r:r + blk + 2 * halo, cols], preferred_element_type=F32)
            acc = xe_ref[halo + r:halo + r + blk, cols].astype(F32) * cw_ref[centre:centre + 1, cols]
            for j, tap in enumerate(side_taps):
                acc = acc + shifted[j * blk:(j + 1) * blk] * cw_ref[tap:tap + 1, cols]
            y = _silu(acc)
            for hc in range(0, PREP_COLS, HEAD_DIM):
                yh = y[:, hc:hc + HEAD_DIM]
                if c + hc < 2 * DN_WIDTH:
                    yh = yh * lax.rsqrt(jnp.sum(yh * yh, axis=-1, keepdims=True) + EPS)
                    if c + hc < DN_WIDTH:
                        yh = yh * (HEAD_DIM ** -0.5)
                qkv_ref[r:r + blk, c + hc:c + hc + HEAD_DIM] = yh.astype(qkv_ref.dtype)


def _dn_prep(zqkv, conv_w):
    b, s, w = zqkv.shape
    tp = PREP_TILE
    halo_blocks = tp // CONV_HALO
    last_halo = s // CONV_HALO - 1
    return pl.pallas_call(
        _dn_prep_kernel,
        grid=(b, s // tp),
        in_specs=[
            pl.BlockSpec((None, tp, w), lambda bi, i: (bi, i, 0)),
            pl.BlockSpec((None, CONV_HALO, w), lambda bi, i: (bi, jnp.maximum(i * halo_blocks - 1, 0), 0)),
            pl.BlockSpec((None, CONV_HALO, w), lambda bi, i: (bi, jnp.minimum((i + 1) * halo_blocks, last_halo), 0)),
            _resident(conv_w.shape),
        ],
        out_specs=pl.BlockSpec((None, tp, w), lambda bi, i: (bi, i, 0)),
        out_shape=jax.ShapeDtypeStruct((b, s, w), MXU_DTYPE),
        scratch_shapes=[pltpu.VMEM((tp + 2 * CONV_HALO, w), zqkv.dtype)],
        compiler_params=_params(2),
        name="dn_prep",
    )(zqkv, zqkv, zqkv, conv_w)


def _unit_triangular_inverses(a_list, row, col):
    c = a_list[0].shape[0]
    shift = int(math.log2(INV_BASE))
    dot = lambda x, y: jnp.dot(x, y, preferred_element_type=F32)
    lo = lambda x: x.astype(MXU_DTYPE)
    eye = (row == col).astype(F32)
    diag = (row >> shift) == (col >> shift)
    a_lo = [lo(a) for a in a_list]
    powers = [jnp.where(diag, a, 0.0) for a in a_list]
    ts = [eye - p for p in powers]
    powers = [lo(p) for p in powers]
    for _ in range(shift - 1):
        powers = [lo(dot(p, p)) for p in powers]
        ts = [t + dot(lo(t), p) for t, p in zip(ts, powers)]
    while (1 << shift) < c:
        pair = lo(((row >> (shift + 1)) == (col >> (shift + 1))) & ((row >> shift) != (col >> shift)))
        ts_lo = [lo(t) for t in ts]
        ys = [lo(dot(a * pair, t)) for a, t in zip(a_lo, ts_lo)]
        ts = [t - dot(t_lo, y) for t, t_lo, y in zip(ts, ts_lo, ys)]
        shift += 1
    return ts


def _dn_intra_kernel(qkv_ref, abc_ref, abr_ref, alog_c_ref, dtb_c_ref, alog_r_ref, dtb_r_ref,
                     blob_f_ref, blob_b_ref, gl_ref):
    c, nh = DN_CHUNK, DN_HEADS
    n_chunks = qkv_ref.shape[0] // c
    row = lax.broadcasted_iota(jnp.int32, (c, c), 0)
    col = lax.broadcasted_iota(jnp.int32, (c, c), 1)
    incl = {False: row >= col, True: row <= col}
    strict = {False: row > col, True: row < col}
    tri = incl[False].astype(MXU_DTYPE)
    fwd_lanes = lax.broadcasted_iota(jnp.int32, (c, 2 * nh), 1) < nh
    fwd_rows = lax.broadcasted_iota(jnp.int32, (2 * nh, c), 0) < nh
    blobs = {False: blob_f_ref, True: blob_b_ref}

    ab_c = abc_ref[...]
    g_c_all = -jnp.exp(alog_c_ref[...]) * _softplus(ab_c[:, :2 * nh] + dtb_c_ref[...])
    beta_all = jax.nn.sigmoid(ab_c[:, 2 * nh:])
    g_r_all = -jnp.exp(alog_r_ref[...]) * _softplus(abr_ref[0:2 * nh, :] + dtb_r_ref[...])

    chains, gates = [], []
    for ci in range(n_chunks):
        rows = slice(ci * c, (ci + 1) * c)
        g_c, g_r = g_c_all[rows], g_r_all[:, rows]
        pre_c = sum(jnp.dot(tri, part, preferred_element_type=F32) for part in _split3(g_c))
        pre_r = sum(lax.dot_general(part, tri, (((1,), (1,)), ((), ())), preferred_element_type=F32)
                    for part in _split3(g_r))
        tot_c, tot_r = pre_c[c - 1:c, :], pre_r[:, c - 1:c]
        gc_c = jnp.where(fwd_lanes, pre_c, tot_c - pre_c + g_c)
        gc_r = jnp.where(fwd_rows, pre_r, tot_r - pre_r + g_r)
        gl_ref[ci] = jnp.exp(tot_c)
        gates.append((rows, gc_c, gc_r, tot_c))
    heads = []
    for rows, gc_c, gc_r, tot_c in gates:
        for h in range(nh):
            q = qkv_ref[rows, h * HEAD_DIM:(h + 1) * HEAD_DIM].astype(F32)
            k = qkv_ref[rows, DN_WIDTH + h * HEAD_DIM:DN_WIDTH + (h + 1) * HEAD_DIM].astype(F32)
            v = qkv_ref[rows, 2 * DN_WIDTH + h * HEAD_DIM:2 * DN_WIDTH + (h + 1) * HEAD_DIM].astype(F32)
            heads.append(dict(rows=rows, h=h, q=q, k=k, v=v, kt=k.T, gc_c=gc_c, gc_r=gc_r, tot_c=tot_c))
    qk_kk = [_mm(jnp.concatenate([hd["q"], hd["k"]], axis=0), hd["kt"]) for hd in heads]
    for hd, prod in zip(heads, qk_kk):
        for reverse in (False, True):
            lane = (nh if reverse else 0) + hd["h"]
            chains.append(dict(rows=hd["rows"], h=hd["h"], reverse=reverse, q=hd["q"], k=hd["k"], v=hd["v"],
                               kt=hd["kt"], qk=prod[:c], kk=prod[c:],
                               gcc=hd["gc_c"][:, lane:lane + 1], gcr=hd["gc_r"][lane:lane + 1, :],
                               beta=beta_all[hd["rows"], lane:lane + 1], ge=hd["tot_c"][:, lane:lane + 1]))

    def store(ch, group, value):
        start = group * DN_WIDTH + ch["h"] * HEAD_DIM
        blobs[ch["reverse"]][ch["rows"], start:start + HEAD_DIM] = value.astype(MXU_DTYPE)

    a_list = []
    for ch in chains:
        decay = jnp.exp(jnp.where(incl[ch["reverse"]], ch["gcc"] - ch["gcr"], -1e30))
        a_list.append(jnp.where(strict[ch["reverse"]], ch["beta"] * ch["kk"] * decay, 0.0))
        store(ch, BLOB_P, jnp.where(incl[ch["reverse"]], ch["qk"] * decay, 0.0))
    ts = _unit_triangular_inverses(a_list, row, col)
    for ch, t in zip(chains, ts):
        eg = jnp.exp(ch["gcc"])
        rhs = jnp.concatenate([ch["v"] * ch["beta"], ch["k"] * (ch["beta"] * eg)], axis=1)
        sol = _mm(t, rhs)
        store(ch, BLOB_U, sol[:, :HEAD_DIM])
        store(ch, BLOB_W, sol[:, HEAD_DIM:])
        store(ch, BLOB_QD, ch["q"] * eg)
        store(ch, BLOB_KDT, ch["kt"] * jnp.exp(ch["ge"] - ch["gcr"]))


def _dn_intra(qkv, zab, zab_t, a_log, dt_bias):
    b, s, w = qkv.shape
    rows = INTRA_CHUNKS * DN_CHUNK
    n_ab = zab.shape[-1]
    alog_c, dtb_c = a_log.reshape(1, -1), dt_bias.reshape(1, -1)
    alog_r, dtb_r = a_log.reshape(-1, 1), dt_bias.reshape(-1, 1)
    blob = jax.ShapeDtypeStruct((b, s, BLOB_GROUPS * DN_WIDTH), MXU_DTYPE)
    blob_spec = pl.BlockSpec((None, rows, BLOB_GROUPS * DN_WIDTH), lambda bi, i: (bi, i, 0))
    return pl.pallas_call(
        _dn_intra_kernel,
        grid=(b, s // rows),
        in_specs=[
            pl.BlockSpec((None, rows, w), lambda bi, i: (bi, i, 0)),
            pl.BlockSpec((None, rows, n_ab), lambda bi, i: (bi, i, 0)),
            pl.BlockSpec((None, n_ab, rows), lambda bi, i: (bi, 0, i)),
            _resident(alog_c.shape), _resident(dtb_c.shape), _resident(alog_r.shape), _resident(dtb_r.shape),
        ],
        out_specs=[blob_spec, blob_spec,
                   pl.BlockSpec((None, INTRA_CHUNKS, 1, 2 * DN_HEADS), lambda bi, i: (bi, i, 0, 0))],
        out_shape=[blob, blob, jax.ShapeDtypeStruct((b, s // DN_CHUNK, 1, 2 * DN_HEADS), F32)],
        compiler_params=_params(2),
        name="dn_intra",
    )(qkv, zab, zab_t, alog_c, dtb_c, alog_r, dtb_r)


def _dn_scan_kernel(gl_ref, blob_f_ref, blob_b_ref, of_ref, ob_ref, s_ref):
    g, t, n = pl.program_id(0), pl.program_id(1), pl.num_programs(1)
    c, nh = DN_CHUNK, DN_HEADS
    n_rows = blob_f_ref.shape[0]

    @pl.when(t == 0)
    def _():
        s_ref[...] = jnp.zeros_like(s_ref)

    chains = []
    for r in range(n_rows):
        for reverse, blob_ref, o_ref in ((False, blob_f_ref, of_ref), (True, blob_b_ref, ob_ref)):
            chunk = (g * n_rows + r) * n + (n - 1 - t if reverse else t)
            for h in range(nh):
                lane = (nh if reverse else 0) + h
                chains.append((blob_ref, o_ref, r, h, r * 2 * nh + lane, chunk * (2 * nh) + lane))

    def group(blob_ref, r, g0, h):
        start = g0 * DN_WIDTH + h * HEAD_DIM
        return blob_ref[r, :, start:start + HEAD_DIM]

    def stacked(blob_ref, r, g0, g1, h):
        return jnp.concatenate([group(blob_ref, r, g0, h), group(blob_ref, r, g1, h)], axis=0)

    ws_qs = [_mm(stacked(blob, r, BLOB_W, BLOB_QD, h), s_ref[slot]) for blob, _, r, h, slot, _ in chains]
    v_news = [group(blob, r, BLOB_U, h).astype(F32) - wq[:c] for (blob, _, r, h, _, _), wq in zip(chains, ws_qs)]
    pv_kv = [_mm(stacked(blob, r, BLOB_P, BLOB_KDT, h), v_new)
             for (blob, _, r, h, _, _), v_new in zip(chains, v_news)]
    for (_, o_ref, r, h, slot, gl_index), wq, pk in zip(chains, ws_qs, pv_kv):
        o_ref[r, :, h * HEAD_DIM:(h + 1) * HEAD_DIM] = (wq[c:] + pk[:c]).astype(o_ref.dtype)
        s_ref[slot] = s_ref[slot] * gl_ref[gl_index] + pk[c:]


def _dn_scan(gl_flat, blob_f, blob_b):
    b, s, w = blob_f.shape
    c, rows = DN_CHUNK, SCAN_ROWS
    n = s // c
    fwd = lambda g, t: (g, t, 0)
    bwd = lambda g, t: (g, n - 1 - t, 0)
    o_shape = jax.ShapeDtypeStruct((b, s, DN_WIDTH), MXU_DTYPE)
    return pl.pallas_call(
        _dn_scan_kernel,
        grid=(b // rows, n),
        in_specs=[pl.BlockSpec(memory_space=pltpu.SMEM),
                  pl.BlockSpec((rows, c, w), fwd), pl.BlockSpec((rows, c, w), bwd)],
        out_specs=[pl.BlockSpec((rows, c, DN_WIDTH), fwd), pl.BlockSpec((rows, c, DN_WIDTH), bwd)],
        out_shape=[o_shape, o_shape],
        scratch_shapes=[pltpu.VMEM((rows * 2 * DN_HEADS, HEAD_DIM, HEAD_DIM), F32)],
        compiler_params=_params(2),
        name="dn_scan",
    )(gl_flat, blob_f, blob_b)


def _tail_kernel(h_ref, ygm_ref, of_ref, ob_ref, zg_ref, p_ref, on_ref, wo_ref, n2_ref, wg_ref, wu_ref, wd_ref,
                 pp_ref, pn_ref, gn_ref, gw_ref, gb_ref, fn_ref, out_ref):
    o = of_ref[...].astype(F32) + ob_ref[...].astype(F32)
    parts = [ygm_ref[...]]
    for hd in range(DN_HEADS):
        cols = slice(hd * HEAD_DIM, (hd + 1) * HEAD_DIM)
        y = _rms(o[:, cols], on_ref[...]) * _silu(zg_ref[:, cols].astype(F32))
        parts.append(y.astype(MXU_DTYPE))
    h = h_ref[...] + _mm(jnp.concatenate(parts, axis=1), wo_ref[...])
    h = h + 0.5 * _swiglu(_rms(h, n2_ref[...]).astype(MXU_DTYPE), wg_ref, wu_ref, wd_ref)
    e = _rms(_mm(p_ref[...], pp_ref[...]), pn_ref[...])
    gate = jax.nn.sigmoid(_mm(_rms(h, gn_ref[...]), gw_ref[...]) + gb_ref[...])
    out_ref[...] = _rms(h + gate * e, fn_ref[...])


def _tail(h1, y_gm, o_f, o_b, zg, p, *weights):
    t, d = h1.shape
    tm = TOKEN_TILE
    row = lambda w: pl.BlockSpec((tm, w), lambda i: (i, 0))
    return pl.pallas_call(
        _tail_kernel,
        grid=(t // tm,),
        in_specs=[row(d), row(GM_WIDTH), row(DN_WIDTH), row(DN_WIDTH), row(DN_WIDTH), row(p.shape[1])]
                 + [_resident(w.shape) for w in weights],
        out_specs=row(d),
        out_shape=jax.ShapeDtypeStruct((t, d), F32),
        compiler_params=_params(1),
        name="tail",
    )(h1, y_gm, o_f, o_b, zg, p, *weights)


def _layer(h, p, ffn1_norm, ffn1_w_gate, ffn1_w_up, ffn1_w_down, mix_norm, w_in, gm_ln_g, gm_ln_b, gm_ws, gm_bs,
           dn_conv_w, dn_a_log, dn_dt_bias, dn_out_norm, w_out, ffn2_norm, ffn2_w_gate, ffn2_w_up, ffn2_w_down,
           ple_proj, ple_norm, ple_gate_norm, ple_gate_w, ple_gate_b, out_norm, bsz, seq):
    wdt = lambda w: w.astype(MXU_DTYPE)
    vec = lambda g: g.reshape(1, -1)
    n_main = 2 * GM_WIDTH + 4 * DN_WIDTH
    h1, y_gm, zqkv, zg, zab = _ffn1_mix(h, vec(ffn1_norm), wdt(ffn1_w_gate), wdt(ffn1_w_up), wdt(ffn1_w_down),
                                        vec(mix_norm), wdt(w_in[:, :n_main]), wdt(w_in[:, n_main:]),
                                        gm_ln_g, gm_ln_b, wdt(gm_ws), gm_bs.T)
    qkv = _dn_prep(zqkv.reshape(bsz, seq, -1), dn_conv_w)
    zab = zab.reshape(bsz, seq, -1)
    blob_f, blob_b, gl = _dn_intra(qkv, zab, jnp.swapaxes(zab, 1, 2), dn_a_log, dn_dt_bias)
    o_f, o_b = _dn_scan(gl.reshape(-1), blob_f, blob_b)
    return _tail(h1, y_gm, o_f.reshape(-1, DN_WIDTH), o_b.reshape(-1, DN_WIDTH), zg, p,
                 vec(dn_out_norm), wdt(w_out), vec(ffn2_norm), wdt(ffn2_w_gate), wdt(ffn2_w_up), wdt(ffn2_w_down),
                 wdt(ple_proj), vec(ple_norm), vec(ple_gate_norm), wdt(ple_gate_w), vec(ple_gate_b), vec(out_norm))


def kernel(x, p, ffn1_norm, ffn1_w_gate, ffn1_w_up, ffn1_w_down, mix_norm, w_in, gm_ln_g, gm_ln_b, gm_ws, gm_bs, dn_conv_w, dn_a_log, dn_dt_bias, dn_out_norm, w_out, ffn2_norm, ffn2_w_gate, ffn2_w_up, ffn2_w_down, ple_proj, ple_norm, ple_gate_norm, ple_gate_w, ple_gate_b, final_norm):
    bsz, seq, d = x.shape
    depth = p.shape[0]
    assert depth == 1, "the final rmsnorm is fused into the layer kernel, so exactly one layer is supported"
    per_layer = (ffn1_norm, ffn1_w_gate, ffn1_w_up, ffn1_w_down, mix_norm, w_in, gm_ln_g, gm_ln_b, gm_ws, gm_bs,
                 dn_conv_w, dn_a_log, dn_dt_bias, dn_out_norm, w_out, ffn2_norm, ffn2_w_gate, ffn2_w_up,
                 ffn2_w_down, ple_proj, ple_norm, ple_gate_norm, ple_gate_w, ple_gate_b)
    out = _layer(x.reshape(bsz * seq, d), p[0].reshape(bsz * seq, -1), *(w[0] for w in per_layer),
                 final_norm, bsz, seq)
    return out.reshape(bsz, seq, d)
```
